```python
import jax, jax.numpy as jnp
from jax import lax
import numpy as np

D_MODEL = 4096
BATCH = 1
SEQ = 8192
DEPTH = 2

MIX_WIDTH = D_MODEL
POOL_WIDTH = MIX_WIDTH // 4
POOL_WINDOWS = (2, 4, 8, 16)
N_POOL_GROUPS = len(POOL_WINDOWS)
POOL_GROUP = POOL_WIDTH // N_POOL_GROUPS
CONV_WIDTH = MIX_WIDTH // 4
CONV_KERNEL = 31
RWKV_WIDTH = MIX_WIDTH - POOL_WIDTH - CONV_WIDTH
RWKV_HEAD = 64
RWKV_HEADS = RWKV_WIDTH // RWKV_HEAD
DECAY_LORA = max(32, round(1.8 * RWKV_WIDTH ** 0.5 / 32) * 32)
AAA_LORA = max(32, round(1.8 * RWKV_WIDTH ** 0.5 / 32) * 32)
MV_LORA = max(32, round(1.3 * RWKV_WIDTH ** 0.5 / 32) * 32)
GATE_LORA = max(32, round(0.6 * RWKV_WIDTH ** 0.8 / 32) * 32)
RWKV_COLS = 3 * RWKV_WIDTH + DECAY_LORA + AAA_LORA + GATE_LORA
IN_COLS = POOL_WIDTH + 2 * CONV_WIDTH + RWKV_COLS
N_GROUPS = 8
EXPERTS_PER_GROUP = 8
N_EXPERTS = N_GROUPS * EXPERTS_PER_GROUP
TOP_K = 2
EXPERT_FF = 512
MOE_BLOCK = 128
NORM_EPS = 1e-6
LN_EPS = 1e-5
RWKV_GN_EPS = 64e-5

kernel_name = 'hybrid_pool_conv_rwkv7_hmoe'


def rms_norm(x, gain):
    xf = x.astype(jnp.float32)
    y = xf * lax.rsqrt(jnp.mean(xf * xf, axis=-1, keepdims=True) + NORM_EPS)
    return (y * gain.astype(jnp.float32)).astype(x.dtype)


def layer_norm(x, gain, bias):
    xf = x.astype(jnp.float32)
    mu = jnp.mean(xf, axis=-1, keepdims=True)
    var = jnp.mean(jnp.square(xf - mu), axis=-1, keepdims=True)
    y = (xf - mu) * lax.rsqrt(var + LN_EPS)
    return (y * gain.astype(jnp.float32) + bias.astype(jnp.float32)).astype(x.dtype)


def token_shift(p, mix):
    prev = jnp.pad(p, ((0, 0), (1, 0), (0, 0)))[:, :-1]
    return p + (prev - p) * mix


def pool_mixer(u, w, scale):
    b, t, _ = u.shape
    uf = u.astype(jnp.float32).reshape(b, t, N_POOL_GROUPS, POOL_GROUP)
    cs = jnp.pad(jnp.cumsum(uf, axis=1), ((0, 0), (1, 0), (0, 0), (0, 0)))
    pos = jnp.arange(t)[:, None]
    win = jnp.asarray(POOL_WINDOWS, dtype=jnp.int32)[None, :]
    lo = jnp.maximum(pos + 1 - win, 0)
    grp = jnp.arange(N_POOL_GROUPS)[None, :]
    window_sum = cs[:, 1:] - cs[:, lo, grp]
    count = jnp.minimum(pos + 1, win).astype(jnp.float32)
    mixed = window_sum / count[None, :, :, None] - uf
    y = jnp.einsum('btgc,gcd->btgd', mixed, w.astype(jnp.float32))
    return (y.reshape(b, t, POOL_WIDTH) * scale.astype(jnp.float32)).astype(u.dtype)


def conv_module(u, dw, dw_b, ln_g, ln_b, pw):
    val, gate = jnp.split(u, 2, axis=-1)
    h = val * jax.nn.sigmoid(gate)
    h = lax.conv_general_dilated(h, dw[:, None, :], window_strides=(1,),
                                 padding=[(CONV_KERNEL - 1, 0)],
                                 dimension_numbers=('NWC', 'WIO', 'NWC'),
                                 feature_group_count=CONV_WIDTH) + dw_b
    h = jax.nn.silu(layer_norm(h, ln_g, ln_b))
    return h @ pw


def rwkv7_recurrence(r, decay, k, v, a, b):
    bsz, t, h, n = r.shape

    def step(state, inp):
        r_t, w_t, k_t, v_t, a_t, b_t = inp
        sa = jnp.einsum('bhvk,bhk->bhv', state, a_t)
        state = (state * w_t[:, :, None, :] + sa[..., None] * b_t[:, :, None, :]
                 + v_t[..., None] * k_t[:, :, None, :])
        return state, jnp.einsum('bhvk,bhk->bhv', state, r_t)

    seqs = tuple(jnp.moveaxis(z.astype(jnp.float32), 1, 0) for z in (r, decay, k, v, a, b))
    state0 = jnp.zeros((bsz, h, n, n), jnp.float32)
    _, y = lax.scan(step, state0, seqs)
    return jnp.moveaxis(y, 0, 1)


def rwkv7_mixer(p, w0, w_up, a0, a_up, g_up, k_k, k_a, r_k, ln_g, ln_b, v_first, v_res):
    bsz, t, _ = p.shape
    c = RWKV_WIDTH
    r, k, v = p[..., :c], p[..., c:2 * c], p[..., 2 * c:3 * c]
    o = 3 * c
    xw = p[..., o:o + DECAY_LORA]
    o += DECAY_LORA
    xa = p[..., o:o + AAA_LORA]
    o += AAA_LORA
    xg = p[..., o:o + GATE_LORA]
    w_log = -jax.nn.softplus(-(w0 + jnp.tanh(xw) @ w_up).astype(jnp.float32)) - 0.5
    decay = jnp.exp(-jnp.exp(w_log))
    a = jax.nn.sigmoid(a0 + xa @ a_up)
    g = jax.nn.sigmoid(xg) @ g_up
    if v_res is None:
        v_first = v
    else:
        vd, v0, v_up = v_res
        v = v + (v_first - v) * jax.nn.sigmoid(v0 + vd @ v_up)

    def heads(z):
        return z.reshape(bsz, t, RWKV_HEADS, RWKV_HEAD).astype(jnp.float32)

    kk = heads(k * k_k)
    kk = kk * lax.rsqrt(jnp.maximum(jnp.sum(kk * kk, axis=-1, keepdims=True), 1e-24))
    k = k * (1 + (a - 1) * k_a)
    rh, kh, vh = heads(r), heads(k), heads(v)
    y = rwkv7_recurrence(rh, heads(decay), kh, vh, -kk, kk * heads(a))
    mu = jnp.mean(y, axis=-1, keepdims=True)
    var = jnp.mean(jnp.square(y - mu), axis=-1, keepdims=True)
    y = ((y - mu) * lax.rsqrt(var + RWKV_GN_EPS)).reshape(bsz, t, c)
    y = y * ln_g.astype(jnp.float32) + ln_b.astype(jnp.float32)
    bonus = jnp.sum(rh * kh * r_k.astype(jnp.float32), axis=-1, keepdims=True) * vh
    y = y + bonus.reshape(bsz, t, c)
    return (y * g.astype(jnp.float32)).astype(p.dtype), v_first


def hier_moe(h, wg, bg, we, be, w_in, w_out):
    bsz, t, d = h.shape
    nt = bsz * t
    ht = h.reshape(nt, d)
    g_prob = jax.nn.softmax((ht @ wg).astype(jnp.float32) + bg.astype(jnp.float32), axis=-1)
    p_grp, grp = lax.top_k(g_prob, 1)
    e_all = ((ht @ we).astype(jnp.float32) + be.astype(jnp.float32)).reshape(nt, N_GROUPS, EXPERTS_PER_GROUP)
    e_prob = jax.nn.softmax(e_all[jnp.arange(nt), grp[:, 0]], axis=-1)
    top_p, top_i = lax.top_k(e_prob, TOP_K)
    gate = p_grp * top_p / jnp.sum(top_p, axis=-1, keepdims=True)
    eid = (grp * EXPERTS_PER_GROUP + top_i).reshape(-1)
    tok = jnp.repeat(jnp.arange(nt, dtype=jnp.int32), TOP_K)
    gate = gate.reshape(-1)
    n_slots = nt * TOP_K
    order = jnp.argsort(eid)
    e_sorted, tok_sorted, gate_sorted = eid[order], tok[order], gate[order]
    counts = jnp.bincount(eid, length=N_EXPERTS)
    padded = (counts + MOE_BLOCK - 1) // MOE_BLOCK * MOE_BLOCK
    pad_end = jnp.cumsum(padded)
    pad_start = pad_end - padded
    start = jnp.cumsum(counts) - counts
    dest = pad_start[e_sorted] + jnp.arange(n_slots) - start[e_sorted]
    n_blocks = -(-(n_slots + N_EXPERTS * (MOE_BLOCK - 1)) // MOE_BLOCK)
    slot_tok = jnp.full((n_blocks * MOE_BLOCK,), nt, jnp.int32).at[dest].set(tok_sorted)
    slot_gate = jnp.zeros((n_blocks * MOE_BLOCK,), jnp.float32).at[dest].set(gate_sorted)
    block_e = jnp.minimum(jnp.searchsorted(pad_end, jnp.arange(n_blocks) * MOE_BLOCK, side='right'),
                          N_EXPERTS - 1)
    h_pad = jnp.concatenate([ht, jnp.zeros((1, d), ht.dtype)], axis=0)

    def expert_block(args):
        rows, e = args
        xb = h_pad[rows]
        gt, up = jnp.split(xb @ w_in[e], 2, axis=-1)
        return (jax.nn.silu(gt) * up) @ w_out[e]

    yb = lax.map(expert_block, (slot_tok.reshape(n_blocks, MOE_BLOCK), block_e))
    yb = yb.reshape(-1, d) * slot_gate[:, None].astype(yb.dtype)
    y = jax.ops.segment_sum(yb, slot_tok, num_segments=nt + 1)[:nt]
    return y.reshape(bsz, t, d)


def setup_inputs(seed: int = 0) -> dict:
    key = jax.random.key(seed)
    ks = iter(jax.random.split(key, 40))

    def nrm(shape, scale):
        return jax.random.normal(next(ks), shape, jnp.float32) * scale

    def unif(shape, lo, hi):
        return jax.random.uniform(next(ks), shape, jnp.float32, lo, hi)

    L, Lv = DEPTH, DEPTH - 1
    return {
        'x': nrm((BATCH, SEQ, D_MODEL), 1.0),
        'mix_norm': 1.0 + nrm((L, D_MODEL), 0.02),
        'w_in': nrm((L, D_MODEL, IN_COLS), D_MODEL ** -0.5),
        'shift_mix': unif((L, RWKV_COLS), 0.0, 1.0),
        'pool_w': nrm((L, N_POOL_GROUPS, POOL_GROUP, POOL_GROUP), POOL_GROUP ** -0.5),
        'pool_scale': 1.0 + nrm((L, POOL_WIDTH), 0.1),
        'conv_dw': nrm((L, CONV_KERNEL, CONV_WIDTH), CONV_KERNEL ** -0.5),
        'conv_dw_b': nrm((L, CONV_WIDTH), 0.02),
        'conv_ln_g': 1.0 + nrm((L, CONV_WIDTH), 0.02),
        'conv_ln_b': nrm((L, CONV_WIDTH), 0.02),
        'conv_pw': nrm((L, CONV_WIDTH, CONV_WIDTH), CONV_WIDTH ** -0.5),
        'rwkv_w0': unif((L, RWKV_WIDTH), -4.0, 1.0),
        'rwkv_w_up': nrm((L, DECAY_LORA, RWKV_WIDTH), 0.5 * DECAY_LORA ** -0.5),
        'rwkv_a0': nrm((L, RWKV_WIDTH), 0.1),
        'rwkv_a_up': nrm((L, AAA_LORA, RWKV_WIDTH), 0.5 * AAA_LORA ** -0.5),
        'rwkv_g_up': nrm((L, GATE_LORA, RWKV_WIDTH), GATE_LORA ** -0.5),
        'rwkv_k_k': 0.85 + nrm((L, RWKV_WIDTH), 0.05),
        'rwkv_k_a': 1.0 + nrm((L, RWKV_WIDTH), 0.05),
        'rwkv_r_k': nrm((L, RWKV_HEADS, RWKV_HEAD), 0.1),
        'rwkv_ln_g': 1.0 + nrm((L, RWKV_WIDTH), 0.02),
        'rwkv_ln_b': nrm((L, RWKV_WIDTH), 0.02),
        'rwkv_v_down': nrm((Lv, D_MODEL, MV_LORA), D_MODEL ** -0.5),
        'rwkv_v_shift': unif((Lv, MV_LORA), 0.0, 1.0),
        'rwkv_v0': nrm((Lv, RWKV_WIDTH), 0.5),
        'rwkv_v_up': nrm((Lv, MV_LORA, RWKV_WIDTH), 0.5 * MV_LORA ** -0.5),
        'w_out': nrm((L, MIX_WIDTH, D_MODEL), MIX_WIDTH ** -0.5),
        'ffn_norm': 1.0 + nrm((L, D_MODEL), 0.02),
        'router_group_w': nrm((L, D_MODEL, N_GROUPS), D_MODEL ** -0.5),
        'router_group_b': nrm((L, N_GROUPS), 0.01),
        'router_expert_w': nrm((L, D_MODEL, N_EXPERTS), D_MODEL ** -0.5),
        'router_expert_b': nrm((L, N_EXPERTS), 0.01),
        'expert_w_in': nrm((L, N_EXPERTS, D_MODEL, 2 * EXPERT_FF), D_MODEL ** -0.5),
        'expert_w_out': nrm((L, N_EXPERTS, EXPERT_FF, D_MODEL), EXPERT_FF ** -0.5),
        'final_norm': 1.0 + nrm((D_MODEL,), 0.02),
    }


def reference(x, mix_norm, w_in, shift_mix, pool_w, pool_scale, conv_dw, conv_dw_b, conv_ln_g,
              conv_ln_b, conv_pw, rwkv_w0, rwkv_w_up, rwkv_a0, rwkv_a_up, rwkv_g_up, rwkv_k_k,
              rwkv_k_a, rwkv_r_k, rwkv_ln_g, rwkv_ln_b, rwkv_v_down, rwkv_v_shift, rwkv_v0,
              rwkv_v_up, w_out, ffn_norm, router_group_w, router_group_b, router_expert_w,
              router_expert_b, expert_w_in, expert_w_out, final_norm):
    v_first = None
    for i in range(DEPTH):
        xn = rms_norm(x, mix_norm[i])
        proj = xn @ w_in[i]
        u_pool = proj[..., :POOL_WIDTH]
        u_conv = proj[..., POOL_WIDTH:POOL_WIDTH + 2 * CONV_WIDTH]
        u_rwkv = token_shift(proj[..., POOL_WIDTH + 2 * CONV_WIDTH:], shift_mix[i])
        if i == 0:
            v_res = None
        else:
            vd = token_shift(xn @ rwkv_v_down[i - 1], rwkv_v_shift[i - 1])
            v_res = (vd, rwkv_v0[i - 1], rwkv_v_up[i - 1])
        y_pool = pool_mixer(u_pool, pool_w[i], pool_scale[i])
        y_conv = conv_module(u_conv, conv_dw[i], conv_dw_b[i], conv_ln_g[i], conv_ln_b[i], conv_pw[i])
        y_rwkv, v_first = rwkv7_mixer(u_rwkv, rwkv_w0[i], rwkv_w_up[i], rwkv_a0[i], rwkv_a_up[i],
                                      rwkv_g_up[i], rwkv_k_k[i], rwkv_k_a[i], rwkv_r_k[i],
                                      rwkv_ln_g[i], rwkv_ln_b[i], v_first, v_res)
        x = x + jnp.concatenate([y_pool, y_conv, y_rwkv], axis=-1) @ w_out[i]
        x = x + hier_moe(rms_norm(x, ffn_norm[i]), router_group_w[i], router_group_b[i],
                         router_expert_w[i], router_expert_b[i], expert_w_in[i], expert_w_out[i])
    return rms_norm(x, final_norm)
```

```python
import functools
import math

import jax
import jax.numpy as jnp
from jax import lax
from jax.experimental import pallas as pl
from jax.experimental.pallas import tpu as pltpu

F32 = jnp.float32
BF16 = jnp.bfloat16
HIGHEST = lax.Precision.HIGHEST

D_MODEL = 4096
POOL_WINDOWS = (2, 4, 8, 16)
POOL_WIDTH = 1024
POOL_GROUP = 256
CONV_WIDTH = 1024
CONV_KERNEL = 31
RWKV_WIDTH = 2048
RWKV_HEAD = 64
DECAY_LORA = 96
AAA_LORA = 96
MV_LORA = 64
GATE_LORA = 256
TAIL = DECAY_LORA + AAA_LORA + MV_LORA
N_GROUPS = 8
EXPERTS_PER_GROUP = 8
N_EXPERTS = 64
TOP_K = 2
EXPERT_FF = 512
MOE_BLOCK = 128
NORM_EPS = 1e-6
LN_EPS = 1e-5
RWKV_GN_EPS = 64e-5

COL_R, COL_K, COL_V = 0, RWKV_WIDTH, 2 * RWKV_WIDTH
COL_CONV = 3 * RWKV_WIDTH
COL_POOL = COL_CONV + 2 * CONV_WIDTH
COL_TAIL = COL_POOL + POOL_WIDTH
COL_XG = COL_TAIL + TAIL
IN_COLS_PACKED = COL_XG + GATE_LORA

LANES = 128
CHUNK = 64
VMEM_LIMIT = 56 * 1024 * 1024


def _params(sem):
    return pltpu.CompilerParams(dimension_semantics=sem, vmem_limit_bytes=VMEM_LIMIT)


def _dot(a, b):
    return jnp.dot(a.astype(BF16), b.astype(BF16), preferred_element_type=F32)


def _dot_nt(a, b):
    return lax.dot_general(a.astype(BF16), b.astype(BF16), (((1,), (1,)), ((), ())),
                           preferred_element_type=F32)


def _dot_tn(a, b):
    return lax.dot_general(a.astype(BF16), b.astype(BF16), (((0,), (0,)), ((), ())),
                           preferred_element_type=F32)


def _dot_split(a, b):
    hi = a.astype(BF16)
    lo = (a - hi.astype(F32)).astype(BF16)
    return (jnp.dot(hi, b, preferred_element_type=F32) + jnp.dot(lo, b, preferred_element_type=F32))


def _norm_matmul_kernel(x_ref, g_ref, w_ref, o_ref, xn_ref):
    @pl.when(pl.program_id(1) == 0)
    def _():
        x = x_ref[...]
        ms = jnp.mean(x * x, axis=-1, keepdims=True)
        xn_ref[...] = (x * lax.rsqrt(ms + NORM_EPS) * g_ref[...]).astype(BF16)

    o_ref[...] = jnp.dot(xn_ref[...], w_ref[...], preferred_element_type=F32)


def _norm_matmul(x, gain, w, tm=512, tn=512):
    t, d = x.shape
    n = w.shape[1]
    return pl.pallas_call(
        _norm_matmul_kernel,
        grid=(t // tm, n // tn),
        in_specs=[pl.BlockSpec((tm, d), lambda i, j: (i, 0)),
                  pl.BlockSpec((1, d), lambda i, j: (0, 0)),
                  pl.BlockSpec((d, tn), lambda i, j: (0, j))],
        out_specs=pl.BlockSpec((tm, tn), lambda i, j: (i, j)),
        out_shape=jax.ShapeDtypeStruct((t, n), F32),
        scratch_shapes=[pltpu.VMEM((tm, d), BF16)],
        compiler_params=_params(("arbitrary", "arbitrary")),
        name="norm_in_proj",
    )(x, gain.reshape(1, d), w)


def _pool_kernel(u_ref, w_ref, s_ref, o_ref, ext_ref):
    i = pl.program_id(0)
    tm = u_ref.shape[0]
    halo = 16

    @pl.when(i == 0)
    def _():
        ext_ref[0:halo, :] = jnp.zeros((halo, POOL_WIDTH), F32)

    @pl.when(i > 0)
    def _():
        ext_ref[0:halo, :] = ext_ref[tm:tm + halo, :]

    ext_ref[halo:, :] = u_ref[...]
    pos = i * tm + lax.broadcasted_iota(jnp.int32, (tm, 1), 0)
    for g, win in enumerate(POOL_WINDOWS):
        cols = slice(g * POOL_GROUP, (g + 1) * POOL_GROUP)
        u = ext_ref[halo:, cols]
        acc = u
        for dlt in range(1, win):
            acc = acc + ext_ref[pl.ds(halo - dlt, tm), cols]
        count = jnp.minimum(pos + 1, win).astype(F32)
        mixed = acc / count - u
        y = _dot(mixed, w_ref[g])
        o_ref[:, cols] = (y * s_ref[:, cols]).astype(o_ref.dtype)


def _pool_mixer(proj, pool_w, pool_scale, tm=256):
    t = proj.shape[0]
    return pl.pallas_call(
        _pool_kernel,
        grid=(t // tm,),
        in_specs=[pl.BlockSpec((tm, POOL_WIDTH), lambda i: (i, COL_POOL // POOL_WIDTH)),
                  pl.BlockSpec((len(POOL_WINDOWS), POOL_GROUP, POOL_GROUP), lambda i: (0, 0, 0)),
                  pl.BlockSpec((1, POOL_WIDTH), lambda i: (0, 0))],
        out_specs=pl.BlockSpec((tm, POOL_WIDTH), lambda i: (i, 0)),
        out_shape=jax.ShapeDtypeStruct((t, POOL_WIDTH), BF16),
        scratch_shapes=[pltpu.VMEM((tm + 16, POOL_WIDTH), F32)],
        compiler_params=_params(("arbitrary",)),
        name="pool_mixer",
    )(proj, pool_w.astype(BF16), pool_scale.reshape(1, POOL_WIDTH))


CONV_ROWS = 32
CONV_HALO = 32


def _conv_kernel(u_ref, dw_ref, dwb_ref, lng_ref, lnb_ref, pw_ref, o_ref, hext_ref, hln_ref):
    i = pl.program_id(0)
    tm = u_ref.shape[0]

    @pl.when(i == 0)
    def _():
        hext_ref[0:CONV_HALO, :] = jnp.zeros((CONV_HALO, CONV_WIDTH), F32)

    @pl.when(i > 0)
    def _():
        hext_ref[0:CONV_HALO, :] = hext_ref[tm:tm + CONV_HALO, :]

    hext_ref[CONV_HALO:, :] = u_ref[:, :CONV_WIDTH] * jax.nn.sigmoid(u_ref[:, CONV_WIDTH:])
    first_tap = CONV_HALO - (CONV_KERNEL - 1)
    for c in range(tm // CONV_ROWS):
        acc = jnp.broadcast_to(dwb_ref[...], (CONV_ROWS, CONV_WIDTH))
        for j in range(CONV_KERNEL):
            acc = acc + dw_ref[j:j + 1, :] * hext_ref[pl.ds(c * CONV_ROWS + first_tap + j, CONV_ROWS), :]
        mu = jnp.mean(acc, axis=-1, keepdims=True)
        cen = acc - mu
        var = jnp.mean(cen * cen, axis=-1, keepdims=True)
        y = cen * lax.rsqrt(var + LN_EPS) * lng_ref[...] + lnb_ref[...]
        hln_ref[c * CONV_ROWS:(c + 1) * CONV_ROWS, :] = (y * jax.nn.sigmoid(y)).astype(BF16)
    o_ref[...] = jnp.dot(hln_ref[...], pw_ref[...], preferred_element_type=F32).astype(o_ref.dtype)


def _conv_module(proj, dw, dw_b, ln_g, ln_b, pw, tm=256):
    t = proj.shape[0]
    row = lambda v: v.reshape(1, CONV_WIDTH)
    return pl.pallas_call(
        _conv_kernel,
        grid=(t // tm,),
        in_specs=[pl.BlockSpec((tm, 2 * CONV_WIDTH), lambda i: (i, COL_CONV // (2 * CONV_WIDTH))),
                  pl.BlockSpec((CONV_KERNEL, CONV_WIDTH), lambda i: (0, 0)),
                  pl.BlockSpec((1, CONV_WIDTH), lambda i: (0, 0)),
                  pl.BlockSpec((1, CONV_WIDTH), lambda i: (0, 0)),
                  pl.BlockSpec((1, CONV_WIDTH), lambda i: (0, 0)),
                  pl.BlockSpec((CONV_WIDTH, CONV_WIDTH), lambda i: (0, 0))],
        out_specs=pl.BlockSpec((tm, CONV_WIDTH), lambda i: (i, 0)),
        out_shape=jax.ShapeDtypeStruct((t, CONV_WIDTH), BF16),
        scratch_shapes=[pltpu.VMEM((tm + CONV_HALO, CONV_WIDTH), F32),
                        pltpu.VMEM((tm, CONV_WIDTH), BF16)],
        compiler_params=_params(("arbitrary",)),
        name="conv_module",
    )(proj, dw, row(dw_b), row(ln_g), row(ln_b), pw.astype(BF16))


PREP_COLS = 512


def _token_shift(p, carry_ref, mix, first):
    @pl.when(first)
    def _():
        carry_ref[...] = jnp.zeros(carry_ref.shape, F32)

    rows = p.shape[0]
    rolled = pltpu.roll(p, 1, axis=0)
    row = lax.broadcasted_iota(jnp.int32, p.shape, 0)
    prev = jnp.where(row == 0, carry_ref[...], rolled)
    carry_ref[...] = p[rows - 1:rows, :]
    return p + (prev - p) * mix


def _segment_sum(x, seg_ones):
    parts = [jnp.dot(x[:, l * LANES:(l + 1) * LANES], seg_ones, precision=HIGHEST,
                     preferred_element_type=F32) for l in range(x.shape[1] // LANES)]
    return jnp.concatenate(parts, axis=1)


def _rwkv_prep_kernel(has_vres, *refs):
    (r_ref, k_ref, v_ref, tail_ref, xg_ref, mr_ref, mk_ref, mv_ref, mt_ref, mg_ref,
     w0_ref, a0_ref, kk_ref, ka_ref, rk_ref, wup_ref, aup_ref, gup_ref, tri_ref, seg_ref) = refs[:20]
    refs = refs[20:]
    if has_vres:
        v0_ref, vup_ref, vfirst_ref = refs[:3]
        refs = refs[3:]
    (rt_ref, at_ref, bt_ref, kt_ref, bh_ref, kh_ref, vo_ref, c_ref, g_ref, gl_ref,
     cr_ref, ck_ref, cv_ref, ct_ref, cg_ref) = refs

    first = pl.program_id(1) == 0
    tm = r_ref.shape[0]
    r = _token_shift(r_ref[...], cr_ref, mr_ref[...], first)
    k = _token_shift(k_ref[...], ck_ref, mk_ref[...], first)
    v = _token_shift(v_ref[...], cv_ref, mv_ref[...], first)
    tail = _token_shift(tail_ref[...], ct_ref, mt_ref[...], first)
    xg = _token_shift(xg_ref[...], cg_ref, mg_ref[...], first)

    w_pre = w0_ref[...] + _dot(jnp.tanh(tail), wup_ref[...])
    lw = -math.exp(-0.5) * jax.nn.sigmoid(w_pre)
    a_sig = jax.nn.sigmoid(a0_ref[...] + _dot(tail, aup_ref[...]))
    g_ref[...] = _dot(jax.nn.sigmoid(xg), gup_ref[...])
    if has_vres:
        v = v + (vfirst_ref[...] - v) * jax.nn.sigmoid(v0_ref[...] + _dot(tail, vup_ref[...]))
    vo_ref[...] = v

    kk = k * kk_ref[...]
    kk = kk * lax.rsqrt(jnp.maximum(_segment_sum(kk * kk, seg_ref[...]), 1e-24))
    k2 = k * (1.0 + (a_sig - 1.0) * ka_ref[...])
    b = kk * a_sig
    c_ref[...] = _segment_sum(r * k2 * rk_ref[...], seg_ref[...]) * v

    cums = jnp.dot(tri_ref[...], lw, precision=HIGHEST, preferred_element_type=F32)
    lc = cums[:tm]
    lend = cums[tm:]
    e_inv = jnp.exp(-lc)
    e_end = jnp.exp(lend - lc)
    rt_ref[...] = (r * jnp.exp(lc)).astype(BF16)
    at_ref[...] = (-kk * jnp.exp(lc - lw)).astype(BF16)
    bt_ref[...] = (b * e_inv).astype(BF16)
    kt_ref[...] = (k2 * e_inv).astype(BF16)
    bh_ref[...] = (b * e_end).astype(BF16)
    kh_ref[...] = (k2 * e_end).astype(BF16)
    for c in range(tm // CHUNK):
        gl_ref[c] = jnp.exp(lend[c * CHUNK:c * CHUNK + 8, :])


def _rwkv_prep(proj, mixes, p, v_res, tm=256):
    t = proj.shape[0]
    cb = PREP_COLS
    ncb = RWKV_WIDTH // cb
    mix_r, mix_k, mix_v, mix_tail, mix_xg = mixes
    has_vres = v_res is not None

    idx = jnp.arange(tm)
    same_chunk = (idx[:, None] // CHUNK) == (idx[None, :] // CHUNK)
    tri = jnp.concatenate([(same_chunk & (idx[None, :] <= idx[:, None])).astype(F32),
                           same_chunk.astype(F32)], axis=0)
    lane = jnp.arange(LANES)
    seg = ((lane[:, None] // RWKV_HEAD) == (lane[None, :] // RWKV_HEAD)).astype(F32)

    col = lambda base: pl.BlockSpec((tm, cb), lambda j, i, base=base: (i, base // cb + j))
    vec = pl.BlockSpec((1, cb), lambda j, i: (0, j))
    small = lambda base: pl.BlockSpec((tm, TAIL), lambda j, i, base=base: (i, base // TAIL))
    smallvec = pl.BlockSpec((1, TAIL), lambda j, i: (0, 0))
    lora = pl.BlockSpec((TAIL, cb), lambda j, i: (0, j))
    const = lambda shape: pl.BlockSpec(shape, lambda j, i: (0, 0))
    tcol = pl.BlockSpec((tm, cb), lambda j, i: (i, j))

    in_specs = [col(COL_R), col(COL_K), col(COL_V), small(COL_TAIL), small(COL_XG),
                vec, vec, vec, smallvec, smallvec,
                vec, vec, vec, vec, vec, lora, lora, lora, const((2 * tm, tm)), const((LANES, LANES))]
    args = [proj, proj, proj, proj, proj,
            mix_r, mix_k, mix_v, mix_tail, mix_xg,
            p["w0"], p["a0"], p["k_k"], p["k_a"], p["r_k"], p["w_up"], p["a_up"], p["g_up"], tri, seg]
    if has_vres:
        v0, v_up, v_first = v_res
        in_specs += [vec, lora, tcol]
        args += [v0, v_up, v_first]

    bf = jax.ShapeDtypeStruct((t, RWKV_WIDTH), BF16)
    fl = jax.ShapeDtypeStruct((t, RWKV_WIDTH), F32)
    out_shape = [bf] * 6 + [fl] * 3 + [jax.ShapeDtypeStruct((t // CHUNK, 8, RWKV_WIDTH), F32)]
    out_specs = [tcol] * 9 + [pl.BlockSpec((tm // CHUNK, 8, cb), lambda j, i: (i, 0, j))]
    return pl.pallas_call(
        functools.partial(_rwkv_prep_kernel, has_vres),
        grid=(ncb, t // tm),
        in_specs=in_specs,
        out_specs=out_specs,
        out_shape=out_shape,
        scratch_shapes=[pltpu.VMEM((1, cb), F32)] * 3 + [pltpu.VMEM((1, TAIL), F32)] * 2,
        compiler_params=_params(("arbitrary", "arbitrary")),
        name="rwkv_prep",
    )(*args)


REC_ROWS = 256


def _rwkv_chunk_kernel(rt_ref, at_ref, bt_ref, kt_ref, bh_ref, kh_ref, v_ref, c_ref, g_ref, gl_ref,
                       lng_ref, lnb_ref, o_ref, h_ref):
    @pl.when(pl.program_id(1) == 0)
    def _():
        h_ref[...] = jnp.zeros(h_ref.shape, F32)

    n = LANES
    row2 = lax.broadcasted_iota(jnp.int32, (n, n), 0)
    col2 = lax.broadcasted_iota(jnp.int32, (n, n), 1)
    bd_mask = (row2 // RWKV_HEAD) == (col2 // RWKV_HEAD)
    eye = row2 == col2
    seg_mean = jnp.where(bd_mask, 1.0 / RWKV_HEAD, 0.0).astype(BF16)
    tpos = lax.broadcasted_iota(jnp.int32, (CHUNK, n), 0)
    spos = lax.broadcasted_iota(jnp.int32, (CHUNK, n), 1) % RWKV_HEAD
    strict = tpos > spos
    incl = tpos >= spos
    blk16 = (tpos // 16) == (spos // 16)
    blk32 = (tpos // 32) == (spos // 32)

    def bd(x):
        xb = x.astype(BF16)
        return jnp.where(bd_mask, jnp.concatenate([xb, xb], axis=0), jnp.zeros((), BF16))

    for c in range(rt_ref.shape[0] // CHUNK):
        sl = slice(c * CHUNK, (c + 1) * CHUNK)
        rt, at, bt, kt = rt_ref[sl, :], at_ref[sl, :], bt_ref[sl, :], kt_ref[sl, :]
        bh, kh = bh_ref[sl, :], kh_ref[sl, :]
        v = v_ref[sl, :]
        lhs = jnp.concatenate([at, rt], axis=0)
        sb = _dot_nt(lhs, bd(bt))
        sk = _dot_nt(lhs, bd(kt))
        a_ab = jnp.where(strict, sb[:CHUNK], 0.0)
        a_ak = jnp.where(strict, sk[:CHUNK], 0.0)
        a_rb = jnp.where(incl, sb[CHUNK:], 0.0)
        a_rk = jnp.where(incl, sk[CHUNK:], 0.0)

        dg = jnp.where(blk16, a_ab, 0.0)
        tinv = jnp.where(tpos == spos, 1.0, 0.0) + dg
        pw = dg
        for _ in range(3):
            pw = _dot(pw, bd(pw))
            tinv = tinv + _dot(tinv, bd(pw))
        off = jnp.where(blk32 & jnp.logical_not(blk16), a_ab, 0.0)
        tinv = tinv + _dot(_dot(tinv, bd(off)), bd(tinv))
        off = jnp.where(blk32, 0.0, a_ab)
        tinv = tinv + _dot(_dot(tinv, bd(off)), bd(tinv))

        bdv = bd(v)
        wv = _dot(a_ak, bdv)
        at2 = _dot(tinv, bd(at))
        u0 = _dot(tinv, bd(wv))
        rp = rt.astype(F32) + _dot(a_rb, bd(at2))
        y0 = _dot(a_rb, bd(u0)) + _dot(a_rk, bdv)
        m = jnp.where(eye, gl_ref[c, 0:1, :], 0.0) + jnp.where(bd_mask, _dot_tn(bh, at2), 0.0)
        n0 = jnp.where(bd_mask, _dot_tn(bh, u0) + _dot_tn(kh, v), 0.0)

        h = h_ref[...]
        h_hi = h.astype(BF16)
        h_lo = (h - h_hi.astype(F32)).astype(BF16)
        y = _dot(rp, h_hi) + y0
        h_ref[...] = _dot(m, h_hi) + _dot(m, h_lo) + n0

        mean = _dot_split(y, seg_mean)
        cen = y - mean
        var = _dot_split(cen * cen, seg_mean)
        yn = cen * lax.rsqrt(var + RWKV_GN_EPS) * lng_ref[...] + lnb_ref[...]
        o_ref[sl, :] = ((yn + c_ref[sl, :]) * g_ref[sl, :]).astype(o_ref.dtype)


def _rwkv_recurrence(ops, ln_g, ln_b):
    rt, at, bt, kt, bh, kh, v, c, g, gl = ops
    t = rt.shape[0]
    tb = REC_ROWS
    blk = pl.BlockSpec((tb, LANES), lambda hp, i: (i, hp))
    vec = pl.BlockSpec((1, LANES), lambda hp, i: (0, hp))
    return pl.pallas_call(
        _rwkv_chunk_kernel,
        grid=(RWKV_WIDTH // LANES, t // tb),
        in_specs=[blk] * 9 + [pl.BlockSpec((tb // CHUNK, 8, LANES), lambda hp, i: (i, 0, hp)), vec, vec],
        out_specs=blk,
        out_shape=jax.ShapeDtypeStruct((t, RWKV_WIDTH), BF16),
        scratch_shapes=[pltpu.VMEM((LANES, LANES), F32)],
        compiler_params=_params(("arbitrary", "arbitrary")),
        name="rwkv_recurrence",
    )(rt, at, bt, kt, bh, kh, v, c, g, gl, ln_g.reshape(1, RWKV_WIDTH), ln_b.reshape(1, RWKV_WIDTH))


def _out_proj_kernel(yp_ref, yc_ref, yr_ref, w_ref, x_ref, o_ref):
    acc = jnp.dot(yp_ref[...], w_ref[0:POOL_WIDTH, :], preferred_element_type=F32)
    acc = acc + jnp.dot(yc_ref[...], w_ref[POOL_WIDTH:POOL_WIDTH + CONV_WIDTH, :], preferred_element_type=F32)
    acc = acc + jnp.dot(yr_ref[...], w_ref[POOL_WIDTH + CONV_WIDTH:, :], preferred_element_type=F32)
    o_ref[...] = x_ref[...] + acc


def _out_proj(y_pool, y_conv, y_rwkv, w, x, tm=1024, tn=512):
    t, d = x.shape
    return pl.pallas_call(
        _out_proj_kernel,
        grid=(t // tm, d // tn),
        in_specs=[pl.BlockSpec((tm, POOL_WIDTH), lambda i, j: (i, 0)),
                  pl.BlockSpec((tm, CONV_WIDTH), lambda i, j: (i, 0)),
                  pl.BlockSpec((tm, RWKV_WIDTH), lambda i, j: (i, 0)),
                  pl.BlockSpec((d, tn), lambda i, j: (0, j)),
                  pl.BlockSpec((tm, tn), lambda i, j: (i, j))],
        out_specs=pl.BlockSpec((tm, tn), lambda i, j: (i, j)),
        out_shape=jax.ShapeDtypeStruct((t, d), F32),
        compiler_params=_params(("arbitrary", "arbitrary")),
        name="out_proj",
    )(y_pool, y_conv, y_rwkv, w, x)


def _router_kernel(x_ref, g_ref, w_ref, b_ref, h_ref, gate_ref, eid_ref):
    x = x_ref[...]
    ms = jnp.mean(x * x, axis=-1, keepdims=True)
    h = x * lax.rsqrt(ms + NORM_EPS) * g_ref[...]
    h_ref[...] = h
    logits = jnp.dot(h, w_ref[...], precision=HIGHEST, preferred_element_type=F32) + b_ref[...]
    lane = lax.broadcasted_iota(jnp.int32, logits.shape, 1)
    neg = -jnp.inf
    big = jnp.int32(1 << 20)

    gl = jnp.where(lane < N_GROUPS, logits, neg)
    gmax = jnp.max(gl, axis=-1, keepdims=True)
    gidx = jnp.min(jnp.where(gl == gmax, lane, big), axis=-1, keepdims=True)
    p_grp = 1.0 / jnp.sum(jnp.exp(gl - gmax), axis=-1, keepdims=True)

    in_grp = (lane >= N_GROUPS) & (lane < N_GROUPS + N_EXPERTS) & ((lane - N_GROUPS) // EXPERTS_PER_GROUP == gidx)
    el = jnp.where(in_grp, logits, neg)
    m1 = jnp.max(el, axis=-1, keepdims=True)
    i1 = jnp.min(jnp.where(el == m1, lane, big), axis=-1, keepdims=True)
    el2 = jnp.where(lane == i1, neg, el)
    m2 = jnp.max(el2, axis=-1, keepdims=True)
    i2 = jnp.min(jnp.where(el2 == m2, lane, big), axis=-1, keepdims=True)
    e2 = jnp.exp(m2 - m1)
    g1 = p_grp / (1.0 + e2)
    g2 = p_grp * e2 / (1.0 + e2)
    gate_ref[...] = jnp.where(lane == 0, g1, jnp.where(lane == 1, g2, 0.0))
    eid_ref[...] = jnp.where(lane == 0, i1 - N_GROUPS, jnp.where(lane == 1, i2 - N_GROUPS, 0))


def _router(x, gain, wg, bg, we, be, tm=256):
    t, d = x.shape
    pad = LANES - N_GROUPS - N_EXPERTS
    w = jnp.concatenate([wg, we, jnp.zeros((d, pad), F32)], axis=1)
    b = jnp.concatenate([bg, be, jnp.zeros((pad,), F32)]).reshape(1, LANES)
    return pl.pallas_call(
        _router_kernel,
        grid=(t // tm,),
        in_specs=[pl.BlockSpec((tm, d), lambda i: (i, 0)),
                  pl.BlockSpec((1, d), lambda i: (0, 0)),
                  pl.BlockSpec((d, LANES), lambda i: (0, 0)),
                  pl.BlockSpec((1, LANES), lambda i: (0, 0))],
        out_specs=[pl.BlockSpec((tm, d), lambda i: (i, 0)),
                   pl.BlockSpec((tm, LANES), lambda i: (i, 0)),
                   pl.BlockSpec((tm, LANES), lambda i: (i, 0))],
        out_shape=[jax.ShapeDtypeStruct((t, d), F32),
                   jax.ShapeDtypeStruct((t, LANES), F32),
                   jax.ShapeDtypeStruct((t, LANES), jnp.int32)],
        compiler_params=_params(("arbitrary",)),
        name="moe_router",
    )(x, gain.reshape(1, d), w, b)


def _row_gather(idx_ref, base, src_hbm, dst, sem, rows, start):
    def body(r, carry):
        cp = pltpu.make_async_copy(src_hbm.at[pl.ds(idx_ref[base + r], 1), :], dst.at[pl.ds(r, 1), :], sem)
        if start:
            cp.start()
        else:
            cp.wait()
        return carry

    lax.fori_loop(0, rows, body, 0)


def _expert_kernel(be_ref, used_ref, tok_ref, h_hbm, win_ref, wout_ref, o_ref, buf_ref, sem_ref):
    b = pl.program_id(0)
    nb = pl.num_programs(0)
    slot = b % 2

    @pl.when(b == 0)
    def _():
        _row_gather(tok_ref, 0, h_hbm, buf_ref.at[0], sem_ref.at[0], MOE_BLOCK, True)

    @pl.when(b + 1 < nb)
    def _():
        _row_gather(tok_ref, (b + 1) * MOE_BLOCK, h_hbm, buf_ref.at[1 - slot], sem_ref.at[1 - slot],
                    MOE_BLOCK, True)

    _row_gather(tok_ref, b * MOE_BLOCK, h_hbm, buf_ref.at[slot], sem_ref.at[slot], MOE_BLOCK, False)

    @pl.when(b < used_ref[0])
    def _():
        xb = buf_ref[slot].astype(BF16)
        mid = jnp.dot(xb, win_ref[0], preferred_element_type=F32)
        gt, up = mid[:, :EXPERT_FF], mid[:, EXPERT_FF:]
        act = (gt * jax.nn.sigmoid(gt) * up).astype(BF16)
        o_ref[...] = jnp.dot(act, wout_ref[0], preferred_element_type=F32)

    @pl.when(b >= used_ref[0])
    def _():
        o_ref[...] = jnp.zeros(o_ref.shape, F32)


def _experts(h, block_e, n_used, slot_tok, w_in, w_out):
    t, d = h.shape
    n_blocks = block_e.shape[0]
    grid_spec = pltpu.PrefetchScalarGridSpec(
        num_scalar_prefetch=3,
        grid=(n_blocks,),
        in_specs=[pl.BlockSpec(memory_space=pl.ANY),
                  pl.BlockSpec((1, d, 2 * EXPERT_FF), lambda b, be, nu, tok: (be[b], 0, 0)),
                  pl.BlockSpec((1, EXPERT_FF, d), lambda b, be, nu, tok: (be[b], 0, 0))],
        out_specs=pl.BlockSpec((MOE_BLOCK, d), lambda b, be, nu, tok: (b, 0)),
        scratch_shapes=[pltpu.VMEM((2, MOE_BLOCK, d), F32), pltpu.SemaphoreType.DMA((2,))],
    )
    return pl.pallas_call(
        _expert_kernel,
        grid_spec=grid_spec,
        out_shape=jax.ShapeDtypeStruct((n_blocks * MOE_BLOCK, d), F32),
        compiler_params=_params(("arbitrary",)),
        name="moe_experts",
    )(block_e, n_used, slot_tok, h, w_in, w_out)


COMBINE_ROWS = 128


def _combine_kernel(final, pos_ref, yb_hbm, x_ref, gate_ref, fg_ref, o_ref, buf_ref, sem_ref):
    i = pl.program_id(0)
    n = pl.num_programs(0)
    slot = i % 2
    rows = TOP_K * COMBINE_ROWS

    @pl.when(i == 0)
    def _():
        _row_gather(pos_ref, 0, yb_hbm, buf_ref.at[0], sem_ref.at[0], rows, True)

    @pl.when(i + 1 < n)
    def _():
        _row_gather(pos_ref, (i + 1) * rows, yb_hbm, buf_ref.at[1 - slot], sem_ref.at[1 - slot], rows, True)

    _row_gather(pos_ref, i * rows, yb_hbm, buf_ref.at[slot], sem_ref.at[slot], rows, False)

    gate = gate_ref[...]
    y = x_ref[...] + gate[:, 0:1] * buf_ref[slot, 0:COMBINE_ROWS, :] + gate[:, 1:2] * buf_ref[slot, COMBINE_ROWS:, :]
    if final:
        ms = jnp.mean(y * y, axis=-1, keepdims=True)
        y = y * lax.rsqrt(ms + NORM_EPS) * fg_ref[...]
    o_ref[...] = y


def _combine(x, yb, pos, gate, final_gain, final):
    t, d = x.shape
    tm = COMBINE_ROWS
    grid_spec = pltpu.PrefetchScalarGridSpec(
        num_scalar_prefetch=1,
        grid=(t // tm,),
        in_specs=[pl.BlockSpec(memory_space=pl.ANY),
                  pl.BlockSpec((tm, d), lambda i, pos: (i, 0)),
                  pl.BlockSpec((tm, LANES), lambda i, pos: (i, 0)),
                  pl.BlockSpec((1, d), lambda i, pos: (0, 0))],
        out_specs=pl.BlockSpec((tm, d), lambda i, pos: (i, 0)),
        scratch_shapes=[pltpu.VMEM((2, TOP_K * tm, d), F32), pltpu.SemaphoreType.DMA((2,))],
    )
    return pl.pallas_call(
        functools.partial(_combine_kernel, final),
        grid_spec=grid_spec,
        out_shape=jax.ShapeDtypeStruct((t, d), F32),
        compiler_params=_params(("arbitrary",)),
        name="moe_combine",
    )(pos, yb, x, gate, final_gain.reshape(1, d))


def _dispatch_tables(eid, t):
    n_slots = t * TOP_K
    eid = eid.reshape(-1)
    tok = jnp.repeat(jnp.arange(t, dtype=jnp.int32), TOP_K)
    order = jnp.argsort(eid)
    e_sorted, tok_sorted = eid[order], tok[order]
    counts = jnp.bincount(eid, length=N_EXPERTS)
    padded = (counts + MOE_BLOCK - 1) // MOE_BLOCK * MOE_BLOCK
    pad_end = jnp.cumsum(padded)
    pad_start = pad_end - padded
    start = jnp.cumsum(counts) - counts
    dest = (pad_start[e_sorted] + jnp.arange(n_slots) - start[e_sorted]).astype(jnp.int32)
    n_blocks = -(-(n_slots + N_EXPERTS * (MOE_BLOCK - 1)) // MOE_BLOCK)
    slot_tok = jnp.zeros((n_blocks * MOE_BLOCK,), jnp.int32).at[dest].set(tok_sorted)
    block_e = jnp.minimum(jnp.searchsorted(pad_end, jnp.arange(n_blocks) * MOE_BLOCK, side='right'),
                          N_EXPERTS - 1).astype(jnp.int32)
    n_used = (pad_end[-1] // MOE_BLOCK).astype(jnp.int32).reshape(1)
    pos = jnp.zeros((n_slots,), jnp.int32).at[order].set(dest).reshape(t, TOP_K)
    pos = pos.reshape(t // COMBINE_ROWS, COMBINE_ROWS, TOP_K).transpose(0, 2, 1).reshape(-1)
    return slot_tok, block_e, n_used, pos


def kernel(x, mix_norm, w_in, shift_mix, pool_w, pool_scale, conv_dw, conv_dw_b, conv_ln_g, conv_ln_b,
           conv_pw, rwkv_w0, rwkv_w_up, rwkv_a0, rwkv_a_up, rwkv_g_up, rwkv_k_k, rwkv_k_a, rwkv_r_k,
           rwkv_ln_g, rwkv_ln_b, rwkv_v_down, rwkv_v_shift, rwkv_v0, rwkv_v_up, w_out, ffn_norm,
           router_group_w, router_group_b, router_expert_w, router_expert_b, expert_w_in, expert_w_out,
           final_norm):
    bsz, t, d = x.shape
    depth = w_in.shape[0]
    outs = []
    for bi in range(bsz):
        xs = x[bi]
        v_first = None
        for i in range(depth):
            wi = w_in[i]
            o_pool, o_conv, o_rwkv = 0, POOL_WIDTH, POOL_WIDTH + 2 * CONV_WIDTH
            o_tail = o_rwkv + 3 * RWKV_WIDTH
            vdown = rwkv_v_down[i - 1] if i > 0 else jnp.zeros((d, MV_LORA), F32)
            w_packed = jnp.concatenate(
                [wi[:, o_rwkv:o_tail], wi[:, o_conv:o_rwkv], wi[:, o_pool:o_conv],
                 wi[:, o_tail:o_tail + DECAY_LORA + AAA_LORA], vdown,
                 wi[:, o_tail + DECAY_LORA + AAA_LORA:]], axis=1).astype(BF16)
            proj = _norm_matmul(xs, mix_norm[i], w_packed)

            sm = shift_mix[i]
            vshift = rwkv_v_shift[i - 1] if i > 0 else jnp.zeros((MV_LORA,), F32)
            o_lora = 3 * RWKV_WIDTH
            mixes = (sm[0:RWKV_WIDTH].reshape(1, -1), sm[RWKV_WIDTH:2 * RWKV_WIDTH].reshape(1, -1),
                     sm[2 * RWKV_WIDTH:o_lora].reshape(1, -1),
                     jnp.concatenate([sm[o_lora:o_lora + DECAY_LORA + AAA_LORA], vshift]).reshape(1, -1),
                     sm[o_lora + DECAY_LORA + AAA_LORA:].reshape(1, -1))
            zrows = lambda n: jnp.zeros((n, RWKV_WIDTH), F32)
            rowv = lambda v: v.reshape(1, RWKV_WIDTH)
            p = {
                "w0": rowv(rwkv_w0[i]), "a0": rowv(rwkv_a0[i]), "k_k": rowv(rwkv_k_k[i]),
                "k_a": rowv(rwkv_k_a[i]), "r_k": rowv(rwkv_r_k[i]),
                "w_up": jnp.concatenate([rwkv_w_up[i], zrows(TAIL - DECAY_LORA)]).astype(BF16),
                "a_up": jnp.concatenate([zrows(DECAY_LORA), rwkv_a_up[i], zrows(MV_LORA)]).astype(BF16),
                "g_up": rwkv_g_up[i].astype(BF16),
            }
            if i == 0:
                v_res = None
            else:
                v_up = jnp.concatenate([zrows(DECAY_LORA + AAA_LORA), rwkv_v_up[i - 1]]).astype(BF16)
                v_res = (rowv(rwkv_v0[i - 1]), v_up, v_first)
            ops = _rwkv_prep(proj, mixes, p, v_res)
            if i == 0:
                v_first = ops[6]
            y_rwkv = _rwkv_recurrence(ops, rwkv_ln_g[i], rwkv_ln_b[i])
            y_pool = _pool_mixer(proj, pool_w[i], pool_scale[i])
            y_conv = _conv_module(proj, conv_dw[i], conv_dw_b[i], conv_ln_g[i], conv_ln_b[i], conv_pw[i])
            xs = _out_proj(y_pool, y_conv, y_rwkv, w_out[i].astype(BF16), xs)

            h, gate, eid = _router(xs, ffn_norm[i], router_group_w[i], router_group_b[i],
                                   router_expert_w[i], router_expert_b[i])
            slot_tok, block_e, n_used, pos = _dispatch_tables(eid[:, :TOP_K], t)
            yb = _experts(h, block_e, n_used, slot_tok, expert_w_in[i].astype(BF16),
                          expert_w_out[i].astype(BF16))
            xs = _combine(xs, yb, pos, gate, final_norm, final=(i == depth - 1))
        outs.append(xs)
    return jnp.stack(outs, axis=0)
```

```python
import functools
import math

import jax
import jax.numpy as jnp
from jax import lax
from jax.experimental import pallas as pl
from jax.experimental.pallas import tpu as pltpu

F32 = jnp.float32
BF16 = jnp.bfloat16
HIGHEST = lax.Precision.HIGHEST

D_MODEL = 4096
POOL_WINDOWS = (2, 4, 8, 16)
POOL_WIDTH = 1024
POOL_GROUP = 256
CONV_WIDTH = 1024
CONV_KERNEL = 31
RWKV_WIDTH = 2048
RWKV_HEAD = 64
DECAY_LORA = 96
AAA_LORA = 96
MV_LORA = 64
GATE_LORA = 256
TAIL = DECAY_LORA + AAA_LORA + MV_LORA
N_GROUPS = 8
EXPERTS_PER_GROUP = 8
N_EXPERTS = 64
TOP_K = 2
EXPERT_FF = 512
MOE_BLOCK = 128
NORM_EPS = 1e-6
LN_EPS = 1e-5
RWKV_GN_EPS = 64e-5

COL_R, COL_K, COL_V = 0, RWKV_WIDTH, 2 * RWKV_WIDTH
COL_CONV = 3 * RWKV_WIDTH
COL_POOL = COL_CONV + 2 * CONV_WIDTH
COL_TAIL = COL_POOL + POOL_WIDTH
COL_XG = COL_TAIL + TAIL
IN_COLS_PACKED = COL_XG + GATE_LORA

LANES = 128
CHUNK = 64
VMEM_LIMIT = 56 * 1024 * 1024


def _params(sem):
    return pltpu.CompilerParams(dimension_semantics=sem, vmem_limit_bytes=VMEM_LIMIT)


def _dot(a, b):
    return jnp.dot(a.astype(BF16), b.astype(BF16), preferred_element_type=F32)


def _dot_nt(a, b):
    return lax.dot_general(a.astype(BF16), b.astype(BF16), (((1,), (1,)), ((), ())),
                           preferred_element_type=F32)


def _dot_tn(a, b):
    return lax.dot_general(a.astype(BF16), b.astype(BF16), (((0,), (0,)), ((), ())),
                           preferred_element_type=F32)


def _bf16_terms(a, terms):
    parts = []
    for _ in range(terms):
        p = a.astype(BF16)
        parts.append(p)
        a = a - p.astype(F32)
    return parts


def _dot_split(a, b, terms=2):
    return sum(jnp.dot(p, b, preferred_element_type=F32) for p in _bf16_terms(a, terms))


def _dot_split_rhs(a, b, terms):
    return sum(jnp.dot(a, p, preferred_element_type=F32) for p in _bf16_terms(b, terms))


def _norm_matmul_kernel(x_ref, g_ref, w_ref, o_ref, xn_ref):
    @pl.when(pl.program_id(1) == 0)
    def _():
        x = x_ref[...]
        ms = jnp.mean(x * x, axis=-1, keepdims=True)
        xn_ref[...] = (x * lax.rsqrt(ms + NORM_EPS) * g_ref[...]).astype(BF16)

    o_ref[...] = jnp.dot(xn_ref[...], w_ref[...], preferred_element_type=F32)


def _norm_matmul(x, gain, w, tm=512, tn=512):
    t, d = x.shape
    n = w.shape[1]
    return pl.pallas_call(
        _norm_matmul_kernel,
        grid=(t // tm, n // tn),
        in_specs=[pl.BlockSpec((tm, d), lambda i, j: (i, 0)),
                  pl.BlockSpec((1, d), lambda i, j: (0, 0)),
                  pl.BlockSpec((d, tn), lambda i, j: (0, j))],
        out_specs=pl.BlockSpec((tm, tn), lambda i, j: (i, j)),
        out_shape=jax.ShapeDtypeStruct((t, n), F32),
        scratch_shapes=[pltpu.VMEM((tm, d), BF16)],
        compiler_params=_params(("arbitrary", "arbitrary")),
        name="norm_in_proj",
    )(x, gain.reshape(1, d), w)


def _pool_kernel(u_ref, w_ref, s_ref, o_ref, ext_ref):
    i = pl.program_id(0)
    tm = u_ref.shape[0]
    halo = 16

    @pl.when(i == 0)
    def _():
        ext_ref[0:halo, :] = jnp.zeros((halo, POOL_WIDTH), F32)

    @pl.when(i > 0)
    def _():
        ext_ref[0:halo, :] = ext_ref[tm:tm + halo, :]

    ext_ref[halo:, :] = u_ref[...]
    pos = i * tm + lax.broadcasted_iota(jnp.int32, (tm, 1), 0)
    for g, win in enumerate(POOL_WINDOWS):
        cols = slice(g * POOL_GROUP, (g + 1) * POOL_GROUP)
        u = ext_ref[halo:, cols]
        acc = u
        for dlt in range(1, win):
            acc = acc + ext_ref[pl.ds(halo - dlt, tm), cols]
        count = jnp.minimum(pos + 1, win).astype(F32)
        mixed = acc / count - u
        y = _dot(mixed, w_ref[g])
        o_ref[:, cols] = (y * s_ref[:, cols]).astype(o_ref.dtype)


def _pool_mixer(proj, pool_w, pool_scale, tm=256):
    t = proj.shape[0]
    return pl.pallas_call(
        _pool_kernel,
        grid=(t // tm,),
        in_specs=[pl.BlockSpec((tm, POOL_WIDTH), lambda i: (i, COL_POOL // POOL_WIDTH)),
                  pl.BlockSpec((len(POOL_WINDOWS), POOL_GROUP, POOL_GROUP), lambda i: (0, 0, 0)),
                  pl.BlockSpec((1, POOL_WIDTH), lambda i: (0, 0))],
        out_specs=pl.BlockSpec((tm, POOL_WIDTH), lambda i: (i, 0)),
        out_shape=jax.ShapeDtypeStruct((t, POOL_WIDTH), BF16),
        scratch_shapes=[pltpu.VMEM((tm + 16, POOL_WIDTH), F32)],
        compiler_params=_params(("arbitrary",)),
        name="pool_mixer",
    )(proj, pool_w.astype(BF16), pool_scale.reshape(1, POOL_WIDTH))


CONV_ROWS = 32
CONV_HALO = 32


def _conv_kernel(u_ref, dw_ref, dwb_ref, lng_ref, lnb_ref, pw_ref, o_ref, hext_ref, hln_ref):
    i = pl.program_id(0)
    tm = u_ref.shape[0]

    @pl.when(i == 0)
    def _():
        hext_ref[0:CONV_HALO, :] = jnp.zeros((CONV_HALO, CONV_WIDTH), F32)

    @pl.when(i > 0)
    def _():
        hext_ref[0:CONV_HALO, :] = hext_ref[tm:tm + CONV_HALO, :]

    hext_ref[CONV_HALO:, :] = u_ref[:, :CONV_WIDTH] * jax.nn.sigmoid(u_ref[:, CONV_WIDTH:])
    first_tap = CONV_HALO - (CONV_KERNEL - 1)
    for c in range(tm // CONV_ROWS):
        acc = jnp.broadcast_to(dwb_ref[...], (CONV_ROWS, CONV_WIDTH))
        for j in range(CONV_KERNEL):
            acc = acc + dw_ref[j:j + 1, :] * hext_ref[pl.ds(c * CONV_ROWS + first_tap + j, CONV_ROWS), :]
        mu = jnp.mean(acc, axis=-1, keepdims=True)
        cen = acc - mu
        var = jnp.mean(cen * cen, axis=-1, keepdims=True)
        y = cen * lax.rsqrt(var + LN_EPS) * lng_ref[...] + lnb_ref[...]
        hln_ref[c * CONV_ROWS:(c + 1) * CONV_ROWS, :] = (y * jax.nn.sigmoid(y)).astype(BF16)
    o_ref[...] = jnp.dot(hln_ref[...], pw_ref[...], preferred_element_type=F32).astype(o_ref.dtype)


def _conv_module(proj, dw, dw_b, ln_g, ln_b, pw, tm=256):
    t = proj.shape[0]
    row = lambda v: v.reshape(1, CONV_WIDTH)
    return pl.pallas_call(
        _conv_kernel,
        grid=(t // tm,),
        in_specs=[pl.BlockSpec((tm, 2 * CONV_WIDTH), lambda i: (i, COL_CONV // (2 * CONV_WIDTH))),
                  pl.BlockSpec((CONV_KERNEL, CONV_WIDTH), lambda i: (0, 0)),
                  pl.BlockSpec((1, CONV_WIDTH), lambda i: (0, 0)),
                  pl.BlockSpec((1, CONV_WIDTH), lambda i: (0, 0)),
                  pl.BlockSpec((1, CONV_WIDTH), lambda i: (0, 0)),
                  pl.BlockSpec((CONV_WIDTH, CONV_WIDTH), lambda i: (0, 0))],
        out_specs=pl.BlockSpec((tm, CONV_WIDTH), lambda i: (i, 0)),
        out_shape=jax.ShapeDtypeStruct((t, CONV_WIDTH), BF16),
        scratch_shapes=[pltpu.VMEM((tm + CONV_HALO, CONV_WIDTH), F32),
                        pltpu.VMEM((tm, CONV_WIDTH), BF16)],
        compiler_params=_params(("arbitrary",)),
        name="conv_module",
    )(proj, dw, row(dw_b), row(ln_g), row(ln_b), pw.astype(BF16))


PREP_COLS = 512


def _token_shift(p, carry_ref, mix, first):
    @pl.when(first)
    def _():
        carry_ref[...] = jnp.zeros(carry_ref.shape, F32)

    rows = p.shape[0]
    rolled = pltpu.roll(p, 1, axis=0)
    row = lax.broadcasted_iota(jnp.int32, p.shape, 0)
    prev = jnp.where(row == 0, carry_ref[...], rolled)
    carry_ref[...] = p[rows - 1:rows, :]
    return p + (prev - p) * mix


def _segment_sum(x, seg_ones):
    parts = [_dot_split(x[:, l * LANES:(l + 1) * LANES], seg_ones) for l in range(x.shape[1] // LANES)]
    return jnp.concatenate(parts, axis=1)


def _rwkv_prep_kernel(has_vres, *refs):
    (r_ref, k_ref, v_ref, tail_ref, xg_ref, mr_ref, mk_ref, mv_ref, mt_ref, mg_ref,
     w0_ref, a0_ref, kk_ref, ka_ref, rk_ref, wup_ref, aup_ref, gup_ref, tri_ref, seg_ref) = refs[:20]
    refs = refs[20:]
    if has_vres:
        v0_ref, vup_ref, vfirst_ref = refs[:3]
        refs = refs[3:]
    (rt_ref, at_ref, bt_ref, kt_ref, bh_ref, kh_ref, vo_ref, c_ref, g_ref, gl_ref,
     cr_ref, ck_ref, cv_ref, ct_ref, cg_ref) = refs

    first = pl.program_id(1) == 0
    tm = r_ref.shape[0]
    r = _token_shift(r_ref[...], cr_ref, mr_ref[...], first)
    k = _token_shift(k_ref[...], ck_ref, mk_ref[...], first)
    v = _token_shift(v_ref[...], cv_ref, mv_ref[...], first)
    tail = _token_shift(tail_ref[...], ct_ref, mt_ref[...], first)
    xg = _token_shift(xg_ref[...], cg_ref, mg_ref[...], first)

    w_pre = w0_ref[...] + _dot(jnp.tanh(tail), wup_ref[...])
    lw = -math.exp(-0.5) * jax.nn.sigmoid(w_pre)
    a_sig = jax.nn.sigmoid(a0_ref[...] + _dot(tail, aup_ref[...]))
    g_ref[...] = _dot(jax.nn.sigmoid(xg), gup_ref[...])
    if has_vres:
        v = v + (vfirst_ref[...] - v) * jax.nn.sigmoid(v0_ref[...] + _dot(tail, vup_ref[...]))
    vo_ref[...] = v

    kk = k * kk_ref[...]
    kk = kk * lax.rsqrt(jnp.maximum(_segment_sum(kk * kk, seg_ref[...]), 1e-24))
    k2 = k * (1.0 + (a_sig - 1.0) * ka_ref[...])
    b = kk * a_sig
    c_ref[...] = _segment_sum(r * k2 * rk_ref[...], seg_ref[...]) * v

    cums = _dot_split_rhs(tri_ref[...], lw, 3)
    lc = cums[:tm]
    lend = cums[tm:]
    e_inv = jnp.exp(-lc)
    e_end = jnp.exp(lend - lc)
    rt_ref[...] = (r * jnp.exp(lc)).astype(BF16)
    at_ref[...] = (-kk * jnp.exp(lc - lw)).astype(BF16)
    bt_ref[...] = (b * e_inv).astype(BF16)
    kt_ref[...] = (k2 * e_inv).astype(BF16)
    bh_ref[...] = (b * e_end).astype(BF16)
    kh_ref[...] = (k2 * e_end).astype(BF16)
    for c in range(tm // CHUNK):
        gl_ref[c] = jnp.exp(lend[c * CHUNK:c * CHUNK + 8, :])


def _rwkv_prep(proj, mixes, p, v_res, tm=256):
    t = proj.shape[0]
    cb = PREP_COLS
    ncb = RWKV_WIDTH // cb
    mix_r, mix_k, mix_v, mix_tail, mix_xg = mixes
    has_vres = v_res is not None

    idx = jnp.arange(tm)
    same_chunk = (idx[:, None] // CHUNK) == (idx[None, :] // CHUNK)
    tri = jnp.concatenate([(same_chunk & (idx[None, :] <= idx[:, None])).astype(BF16),
                           same_chunk.astype(BF16)], axis=0)
    lane = jnp.arange(LANES)
    seg = ((lane[:, None] // RWKV_HEAD) == (lane[None, :] // RWKV_HEAD)).astype(BF16)

    col = lambda base: pl.BlockSpec((tm, cb), lambda j, i, base=base: (i, base // cb + j))
    vec = pl.BlockSpec((1, cb), lambda j, i: (0, j))
    small = lambda base: pl.BlockSpec((tm, TAIL), lambda j, i, base=base: (i, base // TAIL))
    smallvec = pl.BlockSpec((1, TAIL), lambda j, i: (0, 0))
    lora = pl.BlockSpec((TAIL, cb), lambda j, i: (0, j))
    const = lambda shape: pl.BlockSpec(shape, lambda j, i: (0, 0))
    tcol = pl.BlockSpec((tm, cb), lambda j, i: (i, j))

    in_specs = [col(COL_R), col(COL_K), col(COL_V), small(COL_TAIL), small(COL_XG),
                vec, vec, vec, smallvec, smallvec,
                vec, vec, vec, vec, vec, lora, lora, lora, const((2 * tm, tm)), const((LANES, LANES))]
    args = [proj, proj, proj, proj, proj,
            mix_r, mix_k, mix_v, mix_tail, mix_xg,
            p["w0"], p["a0"], p["k_k"], p["k_a"], p["r_k"], p["w_up"], p["a_up"], p["g_up"], tri, seg]
    if has_vres:
        v0, v_up, v_first = v_res
        in_specs += [vec, lora, tcol]
        args += [v0, v_up, v_first]

    bf = jax.ShapeDtypeStruct((t, RWKV_WIDTH), BF16)
    fl = jax.ShapeDtypeStruct((t, RWKV_WIDTH), F32)
    out_shape = [bf] * 6 + [fl] * 3 + [jax.ShapeDtypeStruct((t // CHUNK, 8, RWKV_WIDTH), F32)]
    out_specs = [tcol] * 9 + [pl.BlockSpec((tm // CHUNK, 8, cb), lambda j, i: (i, 0, j))]
    return pl.pallas_call(
        functools.partial(_rwkv_prep_kernel, has_vres),
        grid=(ncb, t // tm),
        in_specs=in_specs,
        out_specs=out_specs,
        out_shape=out_shape,
        scratch_shapes=[pltpu.VMEM((1, cb), F32)] * 3 + [pltpu.VMEM((1, TAIL), F32)] * 2,
        compiler_params=_params(("arbitrary", "arbitrary")),
        name="rwkv_prep",
    )(*args)


REC_ROWS = 1024


def _rwkv_chunk_kernel(rt_ref, at_ref, bt_ref, kt_ref, bh_ref, kh_ref, v_ref, c_ref, g_ref, gl_ref,
                       lng_ref, lnb_ref, o_ref, h_ref):
    @pl.when(pl.program_id(1) == 0)
    def _():
        h_ref[...] = jnp.zeros(h_ref.shape, F32)

    n = LANES
    row2 = lax.broadcasted_iota(jnp.int32, (n, n), 0)
    col2 = lax.broadcasted_iota(jnp.int32, (n, n), 1)
    bd_mask = (row2 // RWKV_HEAD) == (col2 // RWKV_HEAD)
    eye = row2 == col2
    seg_mean = jnp.where(bd_mask, 1.0 / RWKV_HEAD, 0.0).astype(BF16)
    tpos = lax.broadcasted_iota(jnp.int32, (CHUNK, n), 0)
    spos = lax.broadcasted_iota(jnp.int32, (CHUNK, n), 1) % RWKV_HEAD
    strict = tpos > spos
    incl = tpos >= spos
    blk16 = (tpos // 16) == (spos // 16)
    blk32 = (tpos // 32) == (spos // 32)

    def bd(x):
        xb = x.astype(BF16)
        return jnp.where(bd_mask, jnp.concatenate([xb, xb], axis=0), jnp.zeros((), BF16))

    chunks = range(rt_ref.shape[0] // CHUNK)
    sls = [slice(c * CHUNK, (c + 1) * CHUNK) for c in chunks]
    each = lambda f, *cols: [f(*xs) for xs in zip(*cols)]

    rt = [rt_ref[sl, :] for sl in sls]
    at = [at_ref[sl, :] for sl in sls]
    bdv = [bd(v_ref[sl, :]) for sl in sls]
    lhs = each(lambda a, r: jnp.concatenate([a, r], axis=0), at, rt)
    sb = each(lambda l, sl: _dot_nt(l, bd(bt_ref[sl, :])), lhs, sls)
    sk = each(lambda l, sl: _dot_nt(l, bd(kt_ref[sl, :])), lhs, sls)
    a_ab = [jnp.where(strict, s[:CHUNK], 0.0) for s in sb]
    a_ak = [jnp.where(strict, s[:CHUNK], 0.0) for s in sk]
    a_rb = [jnp.where(incl, s[CHUNK:], 0.0) for s in sb]
    a_rk = [jnp.where(incl, s[CHUNK:], 0.0) for s in sk]

    ident = jnp.where(tpos == spos, 1.0, 0.0)
    dg = [jnp.where(blk16, a, 0.0) for a in a_ab]
    tinv = [ident + d for d in dg]
    pw = each(lambda d: _dot(d, bd(d)), dg)
    for _ in range(2):
        both = each(lambda p, t: _dot(jnp.concatenate([p, t], axis=0), bd(p)), pw, tinv)
        pw = [x[:CHUNK] for x in both]
        tinv = each(lambda t, x: t + x[CHUNK:], tinv, both)
    tinv = each(lambda t, p: t + _dot(t, bd(p)), tinv, pw)
    for mask in (blk32 & jnp.logical_not(blk16), jnp.logical_not(blk32)):
        left = each(lambda t, a: _dot(t, bd(jnp.where(mask, a, 0.0))), tinv, a_ab)
        tinv = each(lambda t, l: t + _dot(l, bd(t)), tinv, left)

    akv = each(lambda ak, rk, bv: _dot(jnp.concatenate([ak, rk], axis=0), bv), a_ak, a_rk, bdv)
    at2 = each(lambda t, a: _dot(t, bd(a)), tinv, at)
    u0 = each(lambda t, x: _dot(t, bd(x[:CHUNK])), tinv, akv)
    rp = each(lambda r, a, x: r.astype(F32) + _dot(a, bd(x)), rt, a_rb, at2)
    y0 = each(lambda arb, u, x: _dot(arb, bd(u)) + x[CHUNK:], a_rb, u0, akv)
    m = [jnp.where(eye, gl_ref[c, 0:1, :], 0.0) + jnp.where(bd_mask, _dot_tn(bh_ref[sl, :], x), 0.0)
         for c, sl, x in zip(chunks, sls, at2)]
    n0 = [jnp.where(bd_mask, _dot_tn(bh_ref[sl, :], u) + _dot_tn(kh_ref[sl, :], v_ref[sl, :]), 0.0)
          for sl, u in zip(sls, u0)]

    h = h_ref[...]
    ys = []
    for c in chunks:
        h_hi = h.astype(BF16)
        h_lo = (h - h_hi.astype(F32)).astype(BF16)
        both = _dot(jnp.concatenate([rp[c], m[c]], axis=0), h_hi)
        ys.append(both[:CHUNK] + y0[c])
        h = both[CHUNK:] + _dot(m[c], h_lo) + n0[c]
    h_ref[...] = h

    y = jnp.concatenate(ys, axis=0)
    cen = y - _dot_split(y, seg_mean)
    var = _dot_split(cen * cen, seg_mean)
    yn = cen * lax.rsqrt(var + RWKV_GN_EPS) * lng_ref[...] + lnb_ref[...]
    o_ref[...] = ((yn + c_ref[...]) * g_ref[...]).astype(o_ref.dtype)


def _rwkv_recurrence(ops, ln_g, ln_b):
    rt, at, bt, kt, bh, kh, v, c, g, gl = ops
    t = rt.shape[0]
    tb = REC_ROWS
    blk = pl.BlockSpec((tb, LANES), lambda hp, i: (i, hp))
    vec = pl.BlockSpec((1, LANES), lambda hp, i: (0, hp))
    return pl.pallas_call(
        _rwkv_chunk_kernel,
        grid=(RWKV_WIDTH // LANES, t // tb),
        in_specs=[blk] * 9 + [pl.BlockSpec((tb // CHUNK, 8, LANES), lambda hp, i: (i, 0, hp)), vec, vec],
        out_specs=blk,
        out_shape=jax.ShapeDtypeStruct((t, RWKV_WIDTH), BF16),
        scratch_shapes=[pltpu.VMEM((LANES, LANES), F32)],
        compiler_params=_params(("arbitrary", "arbitrary")),
        name="rwkv_recurrence",
    )(rt, at, bt, kt, bh, kh, v, c, g, gl, ln_g.reshape(1, RWKV_WIDTH), ln_b.reshape(1, RWKV_WIDTH))


def _out_proj_kernel(yp_ref, yc_ref, yr_ref, w_ref, x_ref, o_ref):
    acc = jnp.dot(yp_ref[...], w_ref[0:POOL_WIDTH, :], preferred_element_type=F32)
    acc = acc + jnp.dot(yc_ref[...], w_ref[POOL_WIDTH:POOL_WIDTH + CONV_WIDTH, :], preferred_element_type=F32)
    acc = acc + jnp.dot(yr_ref[...], w_ref[POOL_WIDTH + CONV_WIDTH:, :], preferred_element_type=F32)
    o_ref[...] = x_ref[...] + acc


def _out_proj(y_pool, y_conv, y_rwkv, w, x, tm=1024, tn=512):
    t, d = x.shape
    return pl.pallas_call(
        _out_proj_kernel,
        grid=(t // tm, d // tn),
        in_specs=[pl.BlockSpec((tm, POOL_WIDTH), lambda i, j: (i, 0)),
                  pl.BlockSpec((tm, CONV_WIDTH), lambda i, j: (i, 0)),
                  pl.BlockSpec((tm, RWKV_WIDTH), lambda i, j: (i, 0)),
                  pl.BlockSpec((d, tn), lambda i, j: (0, j)),
                  pl.BlockSpec((tm, tn), lambda i, j: (i, j))],
        out_specs=pl.BlockSpec((tm, tn), lambda i, j: (i, j)),
        out_shape=jax.ShapeDtypeStruct((t, d), F32),
        compiler_params=_params(("arbitrary", "arbitrary")),
        name="out_proj",
    )(y_pool, y_conv, y_rwkv, w, x)


def _router_kernel(x_ref, g_ref, w_ref, b_ref, h_ref, gate_ref, eid_ref):
    x = x_ref[...]
    ms = jnp.mean(x * x, axis=-1, keepdims=True)
    h = x * lax.rsqrt(ms + NORM_EPS) * g_ref[...]
    h_ref[...] = h
    logits = jnp.dot(h, w_ref[...], precision=HIGHEST, preferred_element_type=F32) + b_ref[...]
    lane = lax.broadcasted_iota(jnp.int32, logits.shape, 1)
    neg = -jnp.inf
    big = jnp.int32(1 << 20)

    gl = jnp.where(lane < N_GROUPS, logits, neg)
    gmax = jnp.max(gl, axis=-1, keepdims=True)
    gidx = jnp.min(jnp.where(gl == gmax, lane, big), axis=-1, keepdims=True)
    p_grp = 1.0 / jnp.sum(jnp.exp(gl - gmax), axis=-1, keepdims=True)

    in_grp = (lane >= N_GROUPS) & (lane < N_GROUPS + N_EXPERTS) & ((lane - N_GROUPS) // EXPERTS_PER_GROUP == gidx)
    el = jnp.where(in_grp, logits, neg)
    m1 = jnp.max(el, axis=-1, keepdims=True)
    i1 = jnp.min(jnp.where(el == m1, lane, big), axis=-1, keepdims=True)
    el2 = jnp.where(lane == i1, neg, el)
    m2 = jnp.max(el2, axis=-1, keepdims=True)
    i2 = jnp.min(jnp.where(el2 == m2, lane, big), axis=-1, keepdims=True)
    e2 = jnp.exp(m2 - m1)
    g1 = p_grp / (1.0 + e2)
    g2 = p_grp * e2 / (1.0 + e2)
    gate_ref[...] = jnp.where(lane == 0, g1, jnp.where(lane == 1, g2, 0.0))
    eid_ref[...] = jnp.where(lane == 0, i1 - N_GROUPS, jnp.where(lane == 1, i2 - N_GROUPS, 0))


def _router(x, gain, wg, bg, we, be, tm=256):
    t, d = x.shape
    pad = LANES - N_GROUPS - N_EXPERTS
    w = jnp.concatenate([wg, we, jnp.zeros((d, pad), F32)], axis=1)
    b = jnp.concatenate([bg, be, jnp.zeros((pad,), F32)]).reshape(1, LANES)
    return pl.pallas_call(
        _router_kernel,
        grid=(t // tm,),
        in_specs=[pl.BlockSpec((tm, d), lambda i: (i, 0)),
                  pl.BlockSpec((1, d), lambda i: (0, 0)),
                  pl.BlockSpec((d, LANES), lambda i: (0, 0)),
                  pl.BlockSpec((1, LANES), lambda i: (0, 0))],
        out_specs=[pl.BlockSpec((tm, d), lambda i: (i, 0)),
                   pl.BlockSpec((tm, LANES), lambda i: (i, 0)),
                   pl.BlockSpec((tm, LANES), lambda i: (i, 0))],
        out_shape=[jax.ShapeDtypeStruct((t, d), F32),
                   jax.ShapeDtypeStruct((t, LANES), F32),
                   jax.ShapeDtypeStruct((t, LANES), jnp.int32)],
        compiler_params=_params(("arbitrary",)),
        name="moe_router",
    )(x, gain.reshape(1, d), w, b)


def _row_gather(idx_ref, base, src_hbm, dst, sem, rows, start):
    if not start:
        pltpu.make_async_copy(src_hbm.at[pl.ds(0, rows), :], dst, sem).wait()
        return

    def body(r, carry):
        pltpu.make_async_copy(src_hbm.at[pl.ds(idx_ref[base + r], 1), :], dst.at[pl.ds(r, 1), :], sem).start()
        return carry

    lax.fori_loop(0, rows, body, 0, unroll=8)


def _expert_kernel(be_ref, used_ref, tok_ref, h_hbm, win_ref, wout_ref, o_ref, buf_ref, sem_ref):
    b = pl.program_id(0)
    nb = pl.num_programs(0)
    slot = b % 2

    @pl.when(b == 0)
    def _():
        _row_gather(tok_ref, 0, h_hbm, buf_ref.at[0], sem_ref.at[0], MOE_BLOCK, True)

    @pl.when(b + 1 < nb)
    def _():
        _row_gather(tok_ref, (b + 1) * MOE_BLOCK, h_hbm, buf_ref.at[1 - slot], sem_ref.at[1 - slot],
                    MOE_BLOCK, True)

    _row_gather(tok_ref, b * MOE_BLOCK, h_hbm, buf_ref.at[slot], sem_ref.at[slot], MOE_BLOCK, False)

    @pl.when(b < used_ref[0])
    def _():
        xb = buf_ref[slot].astype(BF16)
        mid = jnp.dot(xb, win_ref[0], preferred_element_type=F32)
        gt, up = mid[:, :EXPERT_FF], mid[:, EXPERT_FF:]
        act = (gt * jax.nn.sigmoid(gt) * up).astype(BF16)
        o_ref[...] = jnp.dot(act, wout_ref[0], preferred_element_type=F32)

    @pl.when(b >= used_ref[0])
    def _():
        o_ref[...] = jnp.zeros(o_ref.shape, F32)


def _experts(h, block_e, n_used, slot_tok, w_in, w_out):
    t, d = h.shape
    n_blocks = block_e.shape[0]
    grid_spec = pltpu.PrefetchScalarGridSpec(
        num_scalar_prefetch=3,
        grid=(n_blocks,),
        in_specs=[pl.BlockSpec(memory_space=pl.ANY),
                  pl.BlockSpec((1, d, 2 * EXPERT_FF), lambda b, be, nu, tok: (be[b], 0, 0)),
                  pl.BlockSpec((1, EXPERT_FF, d), lambda b, be, nu, tok: (be[b], 0, 0))],
        out_specs=pl.BlockSpec((MOE_BLOCK, d), lambda b, be, nu, tok: (b, 0)),
        scratch_shapes=[pltpu.VMEM((2, MOE_BLOCK, d), F32), pltpu.SemaphoreType.DMA((2,))],
    )
    return pl.pallas_call(
        _expert_kernel,
        grid_spec=grid_spec,
        out_shape=jax.ShapeDtypeStruct((n_blocks * MOE_BLOCK, d), F32),
        compiler_params=_params(("arbitrary",)),
        name="moe_experts",
    )(block_e, n_used, slot_tok, h, w_in, w_out)


COMBINE_ROWS = 128


def _combine_kernel(final, pos_ref, yb_hbm, x_ref, gate_ref, fg_ref, o_ref, buf_ref, sem_ref):
    i = pl.program_id(0)
    n = pl.num_programs(0)
    slot = i % 2
    rows = TOP_K * COMBINE_ROWS

    @pl.when(i == 0)
    def _():
        _row_gather(pos_ref, 0, yb_hbm, buf_ref.at[0], sem_ref.at[0], rows, True)

    @pl.when(i + 1 < n)
    def _():
        _row_gather(pos_ref, (i + 1) * rows, yb_hbm, buf_ref.at[1 - slot], sem_ref.at[1 - slot], rows, True)

    _row_gather(pos_ref, i * rows, yb_hbm, buf_ref.at[slot], sem_ref.at[slot], rows, False)

    gate = gate_ref[...]
    y = x_ref[...] + gate[:, 0:1] * buf_ref[slot, 0:COMBINE_ROWS, :] + gate[:, 1:2] * buf_ref[slot, COMBINE_ROWS:, :]
    if final:
        ms = jnp.mean(y * y, axis=-1, keepdims=True)
        y = y * lax.rsqrt(ms + NORM_EPS) * fg_ref[...]
    o_ref[...] = y


def _combine(x, yb, pos, gate, final_gain, final):
    t, d = x.shape
    tm = COMBINE_ROWS
    grid_spec = pltpu.PrefetchScalarGridSpec(
        num_scalar_prefetch=1,
        grid=(t // tm,),
        in_specs=[pl.BlockSpec(memory_space=pl.ANY),
                  pl.BlockSpec((tm, d), lambda i, pos: (i, 0)),
                  pl.BlockSpec((tm, LANES), lambda i, pos: (i, 0)),
                  pl.BlockSpec((1, d), lambda i, pos: (0, 0))],
        out_specs=pl.BlockSpec((tm, d), lambda i, pos: (i, 0)),
        scratch_shapes=[pltpu.VMEM((2, TOP_K * tm, d), F32), pltpu.SemaphoreType.DMA((2,))],
    )
    return pl.pallas_call(
        functools.partial(_combine_kernel, final),
        grid_spec=grid_spec,
        out_shape=jax.ShapeDtypeStruct((t, d), F32),
        compiler_params=_params(("arbitrary",)),
        name="moe_combine",
    )(pos, yb, x, gate, final_gain.reshape(1, d))


def _dispatch_tables(eid, t):
    n_slots = t * TOP_K
    eid = eid.reshape(-1)
    tok = jnp.repeat(jnp.arange(t, dtype=jnp.int32), TOP_K)
    order = jnp.argsort(eid)
    e_sorted, tok_sorted = eid[order], tok[order]
    counts = jnp.bincount(eid, length=N_EXPERTS)
    padded = (counts + MOE_BLOCK - 1) // MOE_BLOCK * MOE_BLOCK
    pad_end = jnp.cumsum(padded)
    pad_start = pad_end - padded
    start = jnp.cumsum(counts) - counts
    dest = (pad_start[e_sorted] + jnp.arange(n_slots) - start[e_sorted]).astype(jnp.int32)
    n_blocks = -(-(n_slots + N_EXPERTS * (MOE_BLOCK - 1)) // MOE_BLOCK)
    slot_tok = jnp.zeros((n_blocks * MOE_BLOCK,), jnp.int32).at[dest].set(tok_sorted)
    block_e = jnp.minimum(jnp.searchsorted(pad_end, jnp.arange(n_blocks) * MOE_BLOCK, side='right'),
                          N_EXPERTS - 1).astype(jnp.int32)
    n_used = (pad_end[-1] // MOE_BLOCK).astype(jnp.int32).reshape(1)
    pos = jnp.zeros((n_slots,), jnp.int32).at[order].set(dest).reshape(t, TOP_K)
    pos = pos.reshape(t // COMBINE_ROWS, COMBINE_ROWS, TOP_K).transpose(0, 2, 1).reshape(-1)
    return slot_tok, block_e, n_used, pos


def kernel(x, mix_norm, w_in, shift_mix, pool_w, pool_scale, conv_dw, conv_dw_b, conv_ln_g, conv_ln_b,
           conv_pw, rwkv_w0, rwkv_w_up, rwkv_a0, rwkv_a_up, rwkv_g_up, rwkv_k_k, rwkv_k_a, rwkv_r_k,
           rwkv_ln_g, rwkv_ln_b, rwkv_v_down, rwkv_v_shift, rwkv_v0, rwkv_v_up, w_out, ffn_norm,
           router_group_w, router_group_b, router_expert_w, router_expert_b, expert_w_in, expert_w_out,
           final_norm):
    bsz, t, d = x.shape
    depth = w_in.shape[0]
    outs = []
    for bi in range(bsz):
        xs = x[bi]
        v_first = None
        for i in range(depth):
            wi = w_in[i]
            o_pool, o_conv, o_rwkv = 0, POOL_WIDTH, POOL_WIDTH + 2 * CONV_WIDTH
            o_tail = o_rwkv + 3 * RWKV_WIDTH
            vdown = rwkv_v_down[i - 1] if i > 0 else jnp.zeros((d, MV_LORA), F32)
            w_packed = jnp.concatenate(
                [wi[:, o_rwkv:o_tail], wi[:, o_conv:o_rwkv], wi[:, o_pool:o_conv],
                 wi[:, o_tail:o_tail + DECAY_LORA + AAA_LORA], vdown,
                 wi[:, o_tail + DECAY_LORA + AAA_LORA:]], axis=1).astype(BF16)
            proj = _norm_matmul(xs, mix_norm[i], w_packed)

            sm = shift_mix[i]
            vshift = rwkv_v_shift[i - 1] if i > 0 else jnp.zeros((MV_LORA,), F32)
            o_lora = 3 * RWKV_WIDTH
            mixes = (sm[0:RWKV_WIDTH].reshape(1, -1), sm[RWKV_WIDTH:2 * RWKV_WIDTH].reshape(1, -1),
                     sm[2 * RWKV_WIDTH:o_lora].reshape(1, -1),
                     jnp.concatenate([sm[o_lora:o_lora + DECAY_LORA + AAA_LORA], vshift]).reshape(1, -1),
                     sm[o_lora + DECAY_LORA + AAA_LORA:].reshape(1, -1))
            zrows = lambda n: jnp.zeros((n, RWKV_WIDTH), F32)
            rowv = lambda v: v.reshape(1, RWKV_WIDTH)
            p = {
                "w0": rowv(rwkv_w0[i]), "a0": rowv(rwkv_a0[i]), "k_k": rowv(rwkv_k_k[i]),
                "k_a": rowv(rwkv_k_a[i]), "r_k": rowv(rwkv_r_k[i]),
                "w_up": jnp.concatenate([rwkv_w_up[i], zrows(TAIL - DECAY_LORA)]).astype(BF16),
                "a_up": jnp.concatenate([zrows(DECAY_LORA), rwkv_a_up[i], zrows(MV_LORA)]).astype(BF16),
                "g_up": rwkv_g_up[i].astype(BF16),
            }
            if i == 0:
                v_res = None
            else:
                v_up = jnp.concatenate([zrows(DECAY_LORA + AAA_LORA), rwkv_v_up[i - 1]]).astype(BF16)
                v_res = (rowv(rwkv_v0[i - 1]), v_up, v_first)
            ops = _rwkv_prep(proj, mixes, p, v_res)
            if i == 0:
                v_first = ops[6]
            y_rwkv = _rwkv_recurrence(ops, rwkv_ln_g[i], rwkv_ln_b[i])
            y_pool = _pool_mixer(proj, pool_w[i], pool_scale[i])
            y_conv = _conv_module(proj, conv_dw[i], conv_dw_b[i], conv_ln_g[i], conv_ln_b[i], conv_pw[i])
            xs = _out_proj(y_pool, y_conv, y_rwkv, w_out[i].astype(BF16), xs)

            h, gate, eid = _router(xs, ffn_norm[i], router_group_w[i], router_group_b[i],
                                   router_expert_w[i], router_expert_b[i])
            slot_tok, block_e, n_used, pos = _dispatch_tables(eid[:, :TOP_K], t)
            yb = _experts(h, block_e, n_used, slot_tok, expert_w_in[i].astype(BF16),
                          expert_w_out[i].astype(BF16))
            xs = _combine(xs, yb, pos, gate, final_norm, final=(i == depth - 1))
        outs.append(xs)
    return jnp.stack(outs, axis=0)
```
